```python
import jax, jax.numpy as jnp
from jax import lax
import numpy as np

D_MODEL = 1024
BATCH = 8
SEQ = 2048
DEPTH = 4

GRID_W = 64
CTX_LEN = 256
N_Q_HEADS = 8
N_KV_HEADS = 2
GROUP = N_Q_HEADS // N_KV_HEADS
HEAD_DIM = 64
ATTN_WIDTH = N_Q_HEADS * HEAD_DIM
KV_WIDTH = N_KV_HEADS * HEAD_DIM
CONV_WIDTH = D_MODEL - ATTN_WIDTH
IN_WIDTH = ATTN_WIDTH + 2 * KV_WIDTH + 2 * CONV_WIDTH
CONV_KERNEL = 31
CONV_PAD = CONV_KERNEL // 2
WINDOW = 128
BLOCK_Q = 128
SPAN = BLOCK_Q + 2 * WINDOW
D_FF = -(-8 * D_MODEL // (3 * 256)) * 256
ROPE_BASE = 10000.0
ROPE_HALF = HEAD_DIM // 2
ROPE_FREQS = HEAD_DIM // 4
EPS = 1e-6
NEG_INF = -1e30
SCALE = HEAD_DIM ** -0.5

kernel_name = "hybrid_swa_conformer_dit"


def rmsnorm(x, g):
    xf = x.astype(jnp.float32)
    y = xf * lax.rsqrt(jnp.mean(xf * xf, axis=-1, keepdims=True) + EPS)
    return (y * g.astype(jnp.float32)).astype(x.dtype)


def layernorm(x, g, b):
    xf = x.astype(jnp.float32)
    mu = jnp.mean(xf, axis=-1, keepdims=True)
    var = jnp.mean(jnp.square(xf - mu), axis=-1, keepdims=True)
    y = (xf - mu) * lax.rsqrt(var + EPS)
    return (y * g.astype(jnp.float32) + b.astype(jnp.float32)).astype(x.dtype)


def modulate(h, shift, scale):
    return h * (1 + scale) + shift


def axial_angles(n_tokens):
    rows = n_tokens // GRID_W
    r = jnp.broadcast_to(jnp.arange(rows, dtype=jnp.float32)[:, None], (rows, GRID_W)).reshape(-1)
    col = jnp.broadcast_to(jnp.arange(GRID_W, dtype=jnp.float32)[None, :], (rows, GRID_W)).reshape(-1)
    freqs = ROPE_BASE ** (-jnp.arange(ROPE_FREQS, dtype=jnp.float32) / ROPE_FREQS)
    return r[:, None] * freqs, col[:, None] * freqs


def rope_1d(x, ang):
    cos = jnp.cos(ang)[:, None, :].astype(x.dtype)
    sin = jnp.sin(ang)[:, None, :].astype(x.dtype)
    x1, x2 = x[..., :ROPE_FREQS], x[..., ROPE_FREQS:]
    return jnp.concatenate([x1 * cos - x2 * sin, x1 * sin + x2 * cos], axis=-1)


def rope_2d(x, ang_r, ang_c):
    return jnp.concatenate([rope_1d(x[..., :ROPE_HALF], ang_r), rope_1d(x[..., ROPE_HALF:], ang_c)], axis=-1)


def split_in(p):
    q = p[..., :ATTN_WIDTH]
    k = p[..., ATTN_WIDTH:ATTN_WIDTH + KV_WIDTH]
    v = p[..., ATTN_WIDTH + KV_WIDTH:ATTN_WIDTH + 2 * KV_WIDTH]
    u = p[..., ATTN_WIDTH + 2 * KV_WIDTH:]
    return q, k, v, u


def latent_attention(q, k, v, k_ctx, v_ctx, sink):
    B, N = q.shape[0], q.shape[1]
    L = k_ctx.shape[1]
    nb = N // BLOCK_Q
    qb = q.reshape(B, nb, BLOCK_Q, N_KV_HEADS, GROUP, HEAD_DIM)
    pad = ((0, 0), (WINDOW, WINDOW), (0, 0), (0, 0))
    kp = jnp.pad(k, pad)
    vp = jnp.pad(v, pad)
    idx = jnp.arange(nb)[:, None] * BLOCK_Q + jnp.arange(SPAN)[None, :]
    kb = kp[:, idx]
    vb = vp[:, idx]
    qpos = jnp.arange(nb)[:, None] * BLOCK_Q + jnp.arange(BLOCK_Q)[None, :]
    kpos = idx - WINDOW
    valid = ((jnp.abs(qpos[:, :, None] - kpos[:, None, :]) <= WINDOW)
             & (kpos[:, None, :] >= 0) & (kpos[:, None, :] < N))
    s_loc = jnp.einsum('bnqhgd,bnkhd->bnhgqk', qb, kb).astype(jnp.float32) * SCALE
    s_loc = jnp.where(valid[None, :, None, None], s_loc, NEG_INF)
    s_ctx = jnp.einsum('bnqhgd,bchd->bnhgqc', qb, k_ctx).astype(jnp.float32) * SCALE
    s_sink = jnp.broadcast_to(sink.astype(jnp.float32).reshape(N_KV_HEADS, GROUP, 1, 1),
                              (B, nb, N_KV_HEADS, GROUP, BLOCK_Q, 1))
    p = jax.nn.softmax(jnp.concatenate([s_loc, s_ctx, s_sink], axis=-1), axis=-1).astype(v.dtype)
    o = (jnp.einsum('bnhgqk,bnkhd->bnqhgd', p[..., :SPAN], vb)
         + jnp.einsum('bnhgqc,bchd->bnqhgd', p[..., SPAN:SPAN + L], v_ctx))
    return o.reshape(B, N, ATTN_WIDTH)


def context_attention(q_c, k_c, v_c, sink):
    B, L = q_c.shape[0], q_c.shape[1]
    qg = q_c.reshape(B, L, N_KV_HEADS, GROUP, HEAD_DIM)
    s = jnp.einsum('bqhgd,bkhd->bhgqk', qg, k_c).astype(jnp.float32) * SCALE
    s_sink = jnp.broadcast_to(sink.astype(jnp.float32).reshape(N_KV_HEADS, GROUP, 1, 1),
                              (B, N_KV_HEADS, GROUP, L, 1))
    p = jax.nn.softmax(jnp.concatenate([s, s_sink], axis=-1), axis=-1)[..., :L].astype(v_c.dtype)
    o = jnp.einsum('bhgqk,bkhd->bqhgd', p, v_c)
    return o.reshape(B, L, ATTN_WIDTH)


def conv_module(u, w_dw, b_dw, ln_g, ln_b):
    z = u[..., :CONV_WIDTH] * jax.nn.sigmoid(u[..., CONV_WIDTH:])
    z = lax.conv_general_dilated(z, w_dw[:, None, :], window_strides=(1,),
                                 padding=((CONV_PAD, CONV_PAD),),
                                 dimension_numbers=('NWC', 'WIO', 'NWC'),
                                 feature_group_count=CONV_WIDTH) + b_dw
    return jax.nn.silu(layernorm(z, ln_g, ln_b))


def merge_groups(a_out, cv_out, g_a, g_c, w_o):
    return jnp.concatenate([rmsnorm(a_out, g_a), rmsnorm(cv_out, g_c)], axis=-1) @ w_o


def swiglu(h, w_gu, w_dn):
    gu = h @ w_gu
    return (jax.nn.silu(gu[..., :D_FF]) * gu[..., D_FF:]) @ w_dn


def setup_inputs(seed: int = 0) -> dict:
    key = jax.random.key(seed)
    ks = jax.random.split(key, 20)
    f32 = jnp.float32
    nrm = lambda k, shape, s: jax.random.normal(k, shape, f32) * s
    return {
        "x": nrm(ks[0], (BATCH, SEQ, D_MODEL), 1.0),
        "c": nrm(ks[1], (BATCH, D_MODEL), 1.0),
        "ctx": nrm(ks[2], (BATCH, CTX_LEN, D_MODEL), 1.0),
        "c_ctx": nrm(ks[3], (D_MODEL,), 1.0),
        "w_ada": nrm(ks[4], (DEPTH, D_MODEL, 6 * D_MODEL), D_MODEL ** -0.5),
        "b_ada": nrm(ks[5], (DEPTH, 6 * D_MODEL), 0.02),
        "g_mix": 1.0 + nrm(ks[6], (DEPTH, D_MODEL), 0.05),
        "w_in": nrm(ks[7], (DEPTH, D_MODEL, IN_WIDTH), D_MODEL ** -0.5),
        "attn_sink": nrm(ks[8], (DEPTH, N_Q_HEADS), 1.0),
        "w_dw": nrm(ks[9], (DEPTH, CONV_KERNEL, CONV_WIDTH), CONV_KERNEL ** -0.5),
        "b_dw": nrm(ks[10], (DEPTH, CONV_WIDTH), 0.02),
        "ln_g": 1.0 + nrm(ks[11], (DEPTH, CONV_WIDTH), 0.05),
        "ln_b": nrm(ks[12], (DEPTH, CONV_WIDTH), 0.02),
        "g_attn_out": 1.0 + nrm(ks[13], (DEPTH, ATTN_WIDTH), 0.05),
        "g_conv_out": 1.0 + nrm(ks[14], (DEPTH, CONV_WIDTH), 0.05),
        "w_out": nrm(ks[15], (DEPTH, D_MODEL, D_MODEL), D_MODEL ** -0.5),
        "g_ffn": 1.0 + nrm(ks[16], (DEPTH, D_MODEL), 0.05),
        "w_gate_up": nrm(ks[17], (DEPTH, D_MODEL, 2 * D_FF), D_MODEL ** -0.5),
        "w_down": nrm(ks[18], (DEPTH, D_FF, D_MODEL), D_FF ** -0.5),
        "g_final": 1.0 + nrm(ks[19], (D_MODEL,), 0.05),
    }


def reference(x, c, ctx, c_ctx, w_ada, b_ada, g_mix, w_in, attn_sink, w_dw, b_dw, ln_g, ln_b,
              g_attn_out, g_conv_out, w_out, g_ffn, w_gate_up, w_down, g_final):
    B, N = x.shape[0], x.shape[1]
    L = ctx.shape[1]
    ang_r, ang_c = axial_angles(N)
    silu_c = jax.nn.silu(c)
    silu_cc = jax.nn.silu(c_ctx)
    for l in range(DEPTH):
        last = l == DEPTH - 1
        mod = (silu_c @ w_ada[l] + b_ada[l])[:, None, :]
        sh1, sc1, gt1, sh2, sc2, gt2 = jnp.split(mod, 6, axis=-1)
        mod_c = silu_cc @ w_ada[l] + b_ada[l]
        csh1, csc1, cgt1, csh2, csc2, cgt2 = jnp.split(mod_c, 6, axis=-1)

        h = modulate(rmsnorm(x, g_mix[l]), sh1, sc1)
        hc = modulate(rmsnorm(ctx, g_mix[l]), csh1, csc1)
        q, k, v, u = split_in(h @ w_in[l])
        q = rope_2d(q.reshape(B, N, N_Q_HEADS, HEAD_DIM), ang_r, ang_c)
        k = rope_2d(k.reshape(B, N, N_KV_HEADS, HEAD_DIM), ang_r, ang_c)
        v = v.reshape(B, N, N_KV_HEADS, HEAD_DIM)
        if last:
            kv_c = hc @ w_in[l][:, ATTN_WIDTH:ATTN_WIDTH + 2 * KV_WIDTH]
            k_c, v_c = kv_c[..., :KV_WIDTH], kv_c[..., KV_WIDTH:]
        else:
            q_c, k_c, v_c, u_c = split_in(hc @ w_in[l])
        k_c = k_c.reshape(B, L, N_KV_HEADS, HEAD_DIM)
        v_c = v_c.reshape(B, L, N_KV_HEADS, HEAD_DIM)

        a_out = latent_attention(q, k, v, k_c, v_c, attn_sink[l])
        cv_out = conv_module(u, w_dw[l], b_dw[l], ln_g[l], ln_b[l])
        x = x + gt1 * merge_groups(a_out, cv_out, g_attn_out[l], g_conv_out[l], w_out[l])

        h2 = modulate(rmsnorm(x, g_ffn[l]), sh2, sc2)
        x = x + gt2 * swiglu(h2, w_gate_up[l], w_down[l])

        if not last:
            a_c = context_attention(q_c.reshape(B, L, N_Q_HEADS, HEAD_DIM), k_c, v_c, attn_sink[l])
            cv_c = conv_module(u_c, w_dw[l], b_dw[l], ln_g[l], ln_b[l])
            ctx = ctx + cgt1 * merge_groups(a_c, cv_c, g_attn_out[l], g_conv_out[l], w_out[l])
            hc2 = modulate(rmsnorm(ctx, g_ffn[l]), csh2, csc2)
            ctx = ctx + cgt2 * swiglu(hc2, w_gate_up[l], w_down[l])
    return rmsnorm(x, g_final)
```

```python
import functools

import jax
import jax.numpy as jnp
from jax import lax
from jax.experimental import pallas as pl
from jax.experimental.pallas import tpu as pltpu

D_MODEL = 1024
DEPTH = 4
GRID_W = 64
N_Q_HEADS = 8
N_KV_HEADS = 2
GROUP = N_Q_HEADS // N_KV_HEADS
HEAD_DIM = 64
ATTN_WIDTH = N_Q_HEADS * HEAD_DIM
KV_WIDTH = N_KV_HEADS * HEAD_DIM
CONV_WIDTH = D_MODEL - ATTN_WIDTH
IN_WIDTH = ATTN_WIDTH + 2 * KV_WIDTH + 2 * CONV_WIDTH
CONV_KERNEL = 31
CONV_PAD = CONV_KERNEL // 2
WINDOW = 128
D_FF = -(-8 * D_MODEL // (3 * 256)) * 256
ROPE_BASE = 10000.0
ROPE_FREQS = HEAD_DIM // 4
EPS = 1e-6
NEG_INF = -1e30
SCALE = HEAD_DIM ** -0.5

LANES = 128
HALO = 16
MOD_ROWS = 16
TQ = 128
SPAN = TQ + 2 * WINDOW
FF_CHUNK = 256
CONV_ROWS = 32
VMEM_LIMIT = 56 * 1024 * 1024

_F32 = jnp.float32
_BF16 = jnp.bfloat16


def _sigmoid(x):
    return 1.0 / (1.0 + jnp.exp(-x))


def _dot(a, b):
    return jnp.dot(a, b, preferred_element_type=_F32)


def _dot_nt(a, b):
    return lax.dot_general(a, b, (((1,), (1,)), ((), ())), preferred_element_type=_F32)


def _rms(x):
    return x * lax.rsqrt(jnp.mean(x * x, axis=-1, keepdims=True) + EPS)


def _ada_kernel(cc_ref, w_ref, b_ref, o_ref):
    cc = cc_ref[...]
    s = (cc * _sigmoid(cc)).astype(_BF16)
    o_ref[0] = _dot(s, w_ref[0].astype(_BF16)) + b_ref[0]


def _ada_mod(cc, w_ada, b_ada):
    return pl.pallas_call(
        _ada_kernel,
        grid=(DEPTH, 6),
        in_specs=[
            pl.BlockSpec((MOD_ROWS, D_MODEL), lambda l, j: (0, 0)),
            pl.BlockSpec((1, D_MODEL, D_MODEL), lambda l, j: (l, 0, j)),
            pl.BlockSpec((1, 1, D_MODEL), lambda l, j: (l, 0, j)),
        ],
        out_specs=pl.BlockSpec((1, MOD_ROWS, D_MODEL), lambda l, j: (l, 0, j)),
        out_shape=jax.ShapeDtypeStruct((DEPTH, MOD_ROWS, 6 * D_MODEL), _F32),
        compiler_params=pltpu.CompilerParams(
            dimension_semantics=("arbitrary", "arbitrary"), vmem_limit_bytes=VMEM_LIMIT),
        name="ada_mod",
    )(cc, w_ada, b_ada.reshape(DEPTH, 1, 6 * D_MODEL))


def _rope(x, cos, sin_signed, lo_mask):
    partner = jnp.where(lo_mask, pltpu.roll(x, LANES - ROPE_FREQS, 1), pltpu.roll(x, ROPE_FREQS, 1))
    return x * cos + partner * sin_signed


def _in_proj_kernel(x_ref, sh_ref, sc_ref, g_ref, w_ref, cos_ref, sin_ref,
                    q_ref, k_ref, v_ref, z_ref, *, rope):
    x = x_ref[...]
    coef = g_ref[...] * (1.0 + sc_ref[0])
    h = (_rms(x) * coef + sh_ref[0]).astype(_BF16)
    q = _dot(h, w_ref[:, :ATTN_WIDTH])
    kv = _dot(h, w_ref[:, ATTN_WIDTH:ATTN_WIDTH + 2 * KV_WIDTH])
    k = kv[:, :KV_WIDTH]
    if rope:
        cos = cos_ref[...]
        sin = sin_ref[...]
        lane = lax.broadcasted_iota(jnp.int32, cos.shape, 1)
        lo = (lane % (2 * ROPE_FREQS)) < ROPE_FREQS
        for a in range(ATTN_WIDTH // LANES):
            sl = slice(a * LANES, (a + 1) * LANES)
            q_ref[:, sl] = _rope(q[:, sl], cos, sin, lo).astype(_BF16)
        k_ref[...] = _rope(k, cos, sin, lo).astype(_BF16)
    else:
        q_ref[...] = q.astype(_BF16)
        k_ref[...] = k.astype(_BF16)
    v_ref[...] = kv[:, KV_WIDTH:].astype(_BF16)
    u0 = ATTN_WIDTH + 2 * KV_WIDTH
    u1 = _dot(h, w_ref[:, u0:u0 + CONV_WIDTH])
    u2 = _dot(h, w_ref[:, u0 + CONV_WIDTH:])
    z_ref[...] = u1 * _sigmoid(u2)


def _in_proj(x, mod, g_mix, w_in, cos, sin, *, seq_len, tm, ctx):
    rows = x.shape[0]
    tiles_per_seq = seq_len // tm
    if ctx:
        mod_row = lambda i: MOD_ROWS // 2
        pos_blk = lambda i: 0
    else:
        mod_row = lambda i: i // tiles_per_seq
        pos_blk = lambda i: i % tiles_per_seq
    const = lambda i: (0, 0)
    return pl.pallas_call(
        functools.partial(_in_proj_kernel, rope=not ctx),
        grid=(rows // tm,),
        in_specs=[
            pl.BlockSpec((tm, D_MODEL), lambda i: (i, 0)),
            pl.BlockSpec((1, 1, D_MODEL), lambda i: (mod_row(i), 0, 0)),
            pl.BlockSpec((1, 1, D_MODEL), lambda i: (mod_row(i), 0, 1)),
            pl.BlockSpec((1, D_MODEL), const),
            pl.BlockSpec((D_MODEL, IN_WIDTH), const),
            pl.BlockSpec((tm, LANES), lambda i: (pos_blk(i), 0)),
            pl.BlockSpec((tm, LANES), lambda i: (pos_blk(i), 0)),
        ],
        out_specs=[
            pl.BlockSpec((tm, ATTN_WIDTH), lambda i: (i, 0)),
            pl.BlockSpec((tm, KV_WIDTH), lambda i: (i, 0)),
            pl.BlockSpec((tm, KV_WIDTH), lambda i: (i, 0)),
            pl.BlockSpec((tm, CONV_WIDTH), lambda i: (i, 0)),
        ],
        out_shape=[
            jax.ShapeDtypeStruct((rows, ATTN_WIDTH), _BF16),
            jax.ShapeDtypeStruct((rows, KV_WIDTH), _BF16),
            jax.ShapeDtypeStruct((rows, KV_WIDTH), _BF16),
            jax.ShapeDtypeStruct((rows, CONV_WIDTH), _F32),
        ],
        compiler_params=pltpu.CompilerParams(
            dimension_semantics=("arbitrary",), vmem_limit_bytes=VMEM_LIMIT),
        name="in_proj_ctx" if ctx else "in_proj",
    )(x, mod, mod, g_mix, w_in, cos, sin)


def _softmax_pv(s_parts, v_parts, sink_ref, o_ref, rows, kv):
    p_parts = [[] for _ in s_parts]
    inv = []
    for h in range(GROUP):
        hs = slice(h * TQ, (h + 1) * TQ)
        sink = sink_ref[kv * GROUP + h]
        m = jnp.full((TQ, 1), sink, _F32)
        for s in s_parts:
            m = jnp.maximum(m, jnp.max(s[hs], axis=-1, keepdims=True))
        den = jnp.exp(sink - m)
        for i, s in enumerate(s_parts):
            p = jnp.exp(s[hs] - m)
            den = den + jnp.sum(p, axis=-1, keepdims=True)
            p_parts[i].append(p.astype(_BF16))
        inv.append(1.0 / den)
    o = None
    for ps, v in zip(p_parts, v_parts):
        t = _dot(jnp.concatenate(ps, axis=0), v)
        o = t if o is None else o + t
    for h in range(GROUP):
        head = kv * GROUP + h
        cols = slice(head * HEAD_DIM, (head + 1) * HEAD_DIM)
        o_ref[rows, cols] = o[h * TQ:(h + 1) * TQ] * inv[h]


def _attn_finish(o_scr, g_ref, out_ref):
    a = o_scr[...]
    out_ref[...] = (_rms(a) * g_ref[...]).astype(_BF16)


def _attn_kernel(sink_ref, q_ref, k_ref, v_ref, kc_ref, vc_ref, g_ref, out_ref, o_scr, *, seq_len, tq_blk):
    j = pl.program_id(1)
    for t in range(tq_blk // TQ):
        q0 = j * tq_blk + t * TQ
        start = pl.multiple_of(jnp.clip(q0 - WINDOW, 0, seq_len - SPAN), TQ)
        shift = q0 - start
        qi = lax.broadcasted_iota(jnp.int32, (GROUP * TQ, SPAN), 0) & (TQ - 1)
        ki = lax.broadcasted_iota(jnp.int32, (GROUP * TQ, SPAN), 1)
        valid = jnp.abs(qi + shift - ki) <= WINDOW
        rows = slice(t * TQ, (t + 1) * TQ)
        for kv in range(N_KV_HEADS):
            kcols = slice(kv * HEAD_DIM, (kv + 1) * HEAD_DIM)
            qg = jnp.concatenate(
                [q_ref[rows, (kv * GROUP + h) * HEAD_DIM:(kv * GROUP + h + 1) * HEAD_DIM]
                 for h in range(GROUP)], axis=0)
            kw = k_ref[pl.ds(start, SPAN), kcols]
            vw = v_ref[pl.ds(start, SPAN), kcols]
            s_loc = jnp.where(valid, _dot_nt(qg, kw), NEG_INF)
            s_ctx = _dot_nt(qg, kc_ref[:, kcols])
            _softmax_pv([s_loc, s_ctx], [vw, vc_ref[:, kcols]], sink_ref, o_scr, rows, kv)
    _attn_finish(o_scr, g_ref, out_ref)


def _attention(sink, q, k, v, kc, vc, g_a, *, batch, seq_len, ctx_len, tq_blk):
    return pl.pallas_call(
        functools.partial(_attn_kernel, seq_len=seq_len, tq_blk=tq_blk),
        grid=(batch, seq_len // tq_blk),
        in_specs=[
            pl.BlockSpec(memory_space=pltpu.SMEM),
            pl.BlockSpec((tq_blk, ATTN_WIDTH), lambda b, j: (b * (seq_len // tq_blk) + j, 0)),
            pl.BlockSpec((seq_len, KV_WIDTH), lambda b, j: (b, 0)),
            pl.BlockSpec((seq_len, KV_WIDTH), lambda b, j: (b, 0)),
            pl.BlockSpec((ctx_len, KV_WIDTH), lambda b, j: (b, 0)),
            pl.BlockSpec((ctx_len, KV_WIDTH), lambda b, j: (b, 0)),
            pl.BlockSpec((1, ATTN_WIDTH), lambda b, j: (0, 0)),
        ],
        out_specs=pl.BlockSpec((tq_blk, ATTN_WIDTH), lambda b, j: (b * (seq_len // tq_blk) + j, 0)),
        out_shape=jax.ShapeDtypeStruct((batch * seq_len, ATTN_WIDTH), _BF16),
        scratch_shapes=[pltpu.VMEM((tq_blk, ATTN_WIDTH), _F32)],
        compiler_params=pltpu.CompilerParams(
            dimension_semantics=("arbitrary", "arbitrary"), vmem_limit_bytes=VMEM_LIMIT),
        name="attention",
    )(sink, q, k, v, kc, vc, g_a)


def _ctx_attn_kernel(sink_ref, q_ref, kc_ref, vc_ref, g_ref, out_ref, o_scr, *, ctx_len):
    for t in range(ctx_len // TQ):
        rows = slice(t * TQ, (t + 1) * TQ)
        for kv in range(N_KV_HEADS):
            kcols = slice(kv * HEAD_DIM, (kv + 1) * HEAD_DIM)
            qg = jnp.concatenate(
                [q_ref[rows, (kv * GROUP + h) * HEAD_DIM:(kv * GROUP + h + 1) * HEAD_DIM]
                 for h in range(GROUP)], axis=0)
            s_ctx = _dot_nt(qg, kc_ref[:, kcols])
            _softmax_pv([s_ctx], [vc_ref[:, kcols]], sink_ref, o_scr, rows, kv)
    _attn_finish(o_scr, g_ref, out_ref)


def _ctx_attention(sink, q, kc, vc, g_a, *, batch, ctx_len):
    return pl.pallas_call(
        functools.partial(_ctx_attn_kernel, ctx_len=ctx_len),
        grid=(batch,),
        in_specs=[
            pl.BlockSpec(memory_space=pltpu.SMEM),
            pl.BlockSpec((ctx_len, ATTN_WIDTH), lambda b: (b, 0)),
            pl.BlockSpec((ctx_len, KV_WIDTH), lambda b: (b, 0)),
            pl.BlockSpec((ctx_len, KV_WIDTH), lambda b: (b, 0)),
            pl.BlockSpec((1, ATTN_WIDTH), lambda b: (0, 0)),
        ],
        out_specs=pl.BlockSpec((ctx_len, ATTN_WIDTH), lambda b: (b, 0)),
        out_shape=jax.ShapeDtypeStruct((batch * ctx_len, ATTN_WIDTH), _BF16),
        scratch_shapes=[pltpu.VMEM((ctx_len, ATTN_WIDTH), _F32)],
        compiler_params=pltpu.CompilerParams(
            dimension_semantics=("arbitrary",), vmem_limit_bytes=VMEM_LIMIT),
        name="ctx_attention",
    )(sink, q, kc, vc, g_a)


def _mix_ffn_kernel(x_ref, a_ref, zp_ref, zc_ref, zn_ref, gt1_ref, sh2_ref, sc2_ref, gt2_ref,
                    wdw_ref, bdw_ref, lng_ref, lnb_ref, gc_ref, wo_ref, gf_ref, wgu_ref, wdn_ref,
                    gfin_ref, out_ref, zs_ref, cv_ref, act_ref, *, tm, tiles_per_seq, final):
    i = pl.program_id(0)
    pos = i % tiles_per_seq
    zs_ref[0:HALO, :] = jnp.where(pos > 0, zp_ref[...], 0.0)
    zs_ref[HALO:HALO + tm, :] = zc_ref[...]
    zs_ref[HALO + tm:, :] = jnp.where(pos < tiles_per_seq - 1, zn_ref[...], 0.0)
    for r in range(tm // CONV_ROWS):
        acc = jnp.broadcast_to(bdw_ref[...], (CONV_ROWS, CONV_WIDTH))
        for k in range(CONV_KERNEL):
            off = r * CONV_ROWS + HALO - CONV_PAD + k
            acc = acc + wdw_ref[k:k + 1, :] * zs_ref[off:off + CONV_ROWS, :]
        mu = jnp.mean(acc, axis=-1, keepdims=True)
        cen = acc - mu
        var = jnp.mean(cen * cen, axis=-1, keepdims=True)
        y = cen * lax.rsqrt(var + EPS) * lng_ref[...] + lnb_ref[...]
        cv = y * _sigmoid(y)
        cv_ref[r * CONV_ROWS:(r + 1) * CONV_ROWS, :] = (_rms(cv) * gc_ref[...]).astype(_BF16)
    mixed = _dot(a_ref[...], wo_ref[:ATTN_WIDTH, :]) + _dot(cv_ref[...], wo_ref[ATTN_WIDTH:, :])
    x1 = x_ref[...] + gt1_ref[0] * mixed
    coef = gf_ref[...] * (1.0 + sc2_ref[0])
    h2 = (_rms(x1) * coef + sh2_ref[0]).astype(_BF16)
    for c in range(D_FF // FF_CHUNK):
        cs = slice(c * FF_CHUNK, (c + 1) * FF_CHUNK)
        g = _dot(h2, wgu_ref[:, c * FF_CHUNK:(c + 1) * FF_CHUNK])
        u = _dot(h2, wgu_ref[:, D_FF + c * FF_CHUNK:D_FF + (c + 1) * FF_CHUNK])
        act_ref[:, cs] = (g * _sigmoid(g) * u).astype(_BF16)
    x2 = x1 + gt2_ref[0] * _dot(act_ref[...], wdn_ref[...])
    if final:
        x2 = _rms(x2) * gfin_ref[...]
    out_ref[...] = x2


def _mix_ffn(x, a, z, mod, w_dw, b_dw, ln_g, ln_b, g_c, w_out, g_ffn, w_gu, w_dn, g_final,
             *, seq_len, tm, ctx, final):
    rows = x.shape[0]
    tiles_per_seq = seq_len // tm
    hb = tm // HALO
    n_hblk = rows // HALO
    mod_row = (lambda i: MOD_ROWS // 2) if ctx else (lambda i: i // tiles_per_seq)
    const = lambda i: (0, 0)
    single = pl.Buffered(1)

    def mod_spec(chunk):
        return pl.BlockSpec((1, 1, D_MODEL), lambda i: (mod_row(i), 0, chunk))

    return pl.pallas_call(
        functools.partial(_mix_ffn_kernel, tm=tm, tiles_per_seq=tiles_per_seq, final=final),
        grid=(rows // tm,),
        in_specs=[
            pl.BlockSpec((tm, D_MODEL), lambda i: (i, 0)),
            pl.BlockSpec((tm, ATTN_WIDTH), lambda i: (i, 0)),
            pl.BlockSpec((HALO, CONV_WIDTH), lambda i: (jnp.maximum(i * hb - 1, 0), 0)),
            pl.BlockSpec((tm, CONV_WIDTH), lambda i: (i, 0)),
            pl.BlockSpec((HALO, CONV_WIDTH), lambda i: (jnp.minimum((i + 1) * hb, n_hblk - 1), 0)),
            mod_spec(2), mod_spec(3), mod_spec(4), mod_spec(5),
            pl.BlockSpec((CONV_KERNEL, CONV_WIDTH), const),
            pl.BlockSpec((1, CONV_WIDTH), const),
            pl.BlockSpec((1, CONV_WIDTH), const),
            pl.BlockSpec((1, CONV_WIDTH), const),
            pl.BlockSpec((1, CONV_WIDTH), const),
            pl.BlockSpec((D_MODEL, D_MODEL), const, pipeline_mode=single),
            pl.BlockSpec((1, D_MODEL), const),
            pl.BlockSpec((D_MODEL, 2 * D_FF), const, pipeline_mode=single),
            pl.BlockSpec((D_FF, D_MODEL), const, pipeline_mode=single),
            pl.BlockSpec((1, D_MODEL), const),
        ],
        out_specs=pl.BlockSpec((tm, D_MODEL), lambda i: (i, 0)),
        out_shape=jax.ShapeDtypeStruct((rows, D_MODEL), _F32),
        scratch_shapes=[
            pltpu.VMEM((tm + 2 * HALO, CONV_WIDTH), _F32),
            pltpu.VMEM((tm, CONV_WIDTH), _BF16),
            pltpu.VMEM((tm, D_FF), _BF16),
        ],
        compiler_params=pltpu.CompilerParams(
            dimension_semantics=("arbitrary",), vmem_limit_bytes=VMEM_LIMIT),
        name="mix_ffn_ctx" if ctx else "mix_ffn",
    )(x, a, z, z, z, mod, mod, mod, mod, w_dw, b_dw, ln_g, ln_b, g_c, w_out, g_ffn, w_gu, w_dn, g_final)


def _rope_tables(n_tokens):
    pos = jnp.arange(n_tokens, dtype=jnp.int32)
    r = (pos // GRID_W).astype(_F32)
    col = (pos % GRID_W).astype(_F32)
    freqs = ROPE_BASE ** (-jnp.arange(ROPE_FREQS, dtype=_F32) / ROPE_FREQS)
    ang_r = r[:, None] * freqs
    ang_c = col[:, None] * freqs
    cos = jnp.concatenate([jnp.cos(ang_r)] * 2 + [jnp.cos(ang_c)] * 2, axis=-1)
    sin = jnp.concatenate([-jnp.sin(ang_r), jnp.sin(ang_r), -jnp.sin(ang_c), jnp.sin(ang_c)], axis=-1)
    reps = LANES // HEAD_DIM
    return jnp.tile(cos, (1, reps)), jnp.tile(sin, (1, reps))


def kernel(x, c, ctx, c_ctx, w_ada, b_ada, g_mix, w_in, attn_sink, w_dw, b_dw, ln_g, ln_b,
           g_attn_out, g_conv_out, w_out, g_ffn, w_gate_up, w_down, g_final):
    batch, seq_len, _ = x.shape
    ctx_len = ctx.shape[1]
    assert batch + 1 <= MOD_ROWS // 2 + 1 and seq_len % 512 == 0 and ctx_len % TQ == 0

    cc = jnp.zeros((MOD_ROWS, D_MODEL), _F32).at[:batch].set(c).at[MOD_ROWS // 2].set(c_ctx)
    mod_all = _ada_mod(cc, w_ada, b_ada)
    cos, sin = _rope_tables(seq_len)

    col_scale = jnp.concatenate([jnp.full((ATTN_WIDTH,), SCALE, _F32),
                                 jnp.ones((IN_WIDTH - ATTN_WIDTH,), _F32)])
    w_in_b = (w_in * col_scale).astype(_BF16)
    w_out_b = w_out.astype(_BF16)
    w_gu_b = w_gate_up.astype(_BF16)
    w_dn_b = w_down.astype(_BF16)

    xl = x.reshape(batch * seq_len, D_MODEL)
    xc = ctx.reshape(batch * ctx_len, D_MODEL)
    row = lambda p, l: p[l].reshape(1, -1)
    for l in range(DEPTH):
        last = l == DEPTH - 1
        mod = mod_all[l].reshape(MOD_ROWS, 1, 6 * D_MODEL)
        gm = row(g_mix, l)
        ga = row(g_attn_out, l)
        q, k, v, z = _in_proj(xl, mod, gm, w_in_b[l], cos, sin, seq_len=seq_len, tm=512, ctx=False)
        qc, kc, vc, zc = _in_proj(xc, mod, gm, w_in_b[l], cos, sin, seq_len=ctx_len, tm=ctx_len, ctx=True)
        a = _attention(attn_sink[l], q, k, v, kc, vc, ga,
                       batch=batch, seq_len=seq_len, ctx_len=ctx_len, tq_blk=512)
        tail = (w_dw[l], row(b_dw, l), row(ln_g, l), row(ln_b, l), row(g_conv_out, l), w_out_b[l],
                row(g_ffn, l), w_gu_b[l], w_dn_b[l], g_final.reshape(1, -1))
        xl = _mix_ffn(xl, a, z, mod, *tail, seq_len=seq_len, tm=512, ctx=False, final=last)
        if not last:
            ac = _ctx_attention(attn_sink[l], qc, kc, vc, ga, batch=batch, ctx_len=ctx_len)
            xc = _mix_ffn(xc, ac, zc, mod, *tail, seq_len=ctx_len, tm=ctx_len, ctx=True, final=False)
    return xl.reshape(batch, seq_len, D_MODEL)
```
